```python
import math
import jax, jax.numpy as jnp
from jax import lax
import numpy as np

D_MODEL = 2048
BATCH = 2
SEQ = 16384
DEPTH = 1

HEAD_DIM = 128
D_ATTN = D_MODEL // 2
N_ATTN_HEADS = D_ATTN // HEAD_DIM
D_CONV = D_MODEL - D_ATTN
N_CONV_GROUPS = D_CONV // HEAD_DIM
D_MIX = D_ATTN + D_CONV
D_IN = 3 * D_ATTN + 2 * D_CONV
CONV_WIDTH = 31
D_FF = 4 * D_MODEL
Q_BLOCK = 128
N_COND = 6
LN_EPS = 1e-5
DEEPNORM_ALPHA = (2.0 * DEPTH) ** 0.25
DEEPNORM_BETA = (8.0 * DEPTH) ** -0.25

kernel_name = "stickbreak_conformer_hybrid_deepnorm_adaln"


def layer_norm_plain(x):
    xf = x.astype(jnp.float32)
    mu = jnp.mean(xf, axis=-1, keepdims=True)
    var = jnp.mean(jnp.square(xf - mu), axis=-1, keepdims=True)
    return ((xf - mu) * lax.rsqrt(var + LN_EPS)).astype(x.dtype)


def layer_norm(x, g, b):
    return layer_norm_plain(x) * g + b


def modulate(x, shift, scale):
    return x * (1.0 + scale[:, None, :]) + shift[:, None, :]


def stick_breaking_attention(q, k, v):
    seq = q.shape[1]
    scale = 1.0 / math.sqrt(q.shape[-1])
    outs = []
    for i in range(seq // Q_BLOCK):
        t0 = i * Q_BLOCK
        kv_len = t0 + Q_BLOCK
        qb = q[:, t0:kv_len]
        kb = k[:, :kv_len]
        vb = v[:, :kv_len]
        z = jnp.einsum('bqhd,bkhd->bhqk', qb, kb).astype(jnp.float32) * scale
        t_idx = t0 + jnp.arange(Q_BLOCK)[:, None]
        s_idx = jnp.arange(kv_len)[None, :]
        mask = s_idx < t_idx
        log_one_minus_beta = jnp.where(mask, -jax.nn.softplus(z), 0.0)
        suffix = lax.cumsum(log_one_minus_beta, axis=3, reverse=True) - log_one_minus_beta
        log_a = jax.nn.log_sigmoid(z) + suffix
        a = jnp.where(mask, jnp.exp(log_a), 0.0).astype(vb.dtype)
        outs.append(jnp.einsum('bhqk,bkhd->bqhd', a, vb))
    return jnp.concatenate(outs, axis=1)


def conformer_conv(val, gate, conv_w, conv_b, ln_g, ln_b):
    u = val * jax.nn.sigmoid(gate)
    y = lax.conv_general_dilated(
        u, conv_w.astype(u.dtype), window_strides=(1,),
        padding=[(CONV_WIDTH - 1, 0)],
        dimension_numbers=('NWC', 'WIO', 'NWC'),
        feature_group_count=D_CONV) + conv_b
    y = layer_norm(y, ln_g, ln_b)
    return jax.nn.silu(y)


def setup_inputs(seed: int = 0) -> dict:
    key = jax.random.key(seed)
    ks = jax.random.split(key, 20)
    f32 = jnp.float32
    nrm = lambda k, shape, s: jax.random.normal(k, shape, f32) * s
    x = jax.random.normal(ks[0], (BATCH, SEQ, D_MODEL), f32)
    c = jax.random.normal(ks[1], (BATCH, D_MODEL), f32)
    w_cond = nrm(ks[2], (DEPTH, D_MODEL, N_COND * D_MODEL), D_MODEL ** -0.5)
    b_cond = nrm(ks[3], (DEPTH, N_COND * D_MODEL), 0.02)
    s_in = D_MODEL ** -0.5
    w_qk = nrm(ks[4], (DEPTH, D_MODEL, 2 * D_ATTN), s_in)
    w_v = nrm(ks[5], (DEPTH, D_MODEL, D_ATTN), s_in * DEEPNORM_BETA)
    w_glu = nrm(ks[6], (DEPTH, D_MODEL, 2 * D_CONV), s_in)
    w_in = jnp.concatenate([w_qk, w_v, w_glu], axis=-1)
    conv_w = nrm(ks[7], (DEPTH, CONV_WIDTH, 1, D_CONV), CONV_WIDTH ** -0.5)
    conv_b = nrm(ks[8], (DEPTH, D_CONV), 0.02)
    conv_ln_g = 1.0 + nrm(ks[9], (DEPTH, D_CONV), 0.02)
    conv_ln_b = nrm(ks[10], (DEPTH, D_CONV), 0.02)
    w_out = nrm(ks[11], (DEPTH, D_MIX, D_MODEL), D_MIX ** -0.5 * DEEPNORM_BETA)
    ln1_g = 1.0 + nrm(ks[12], (DEPTH, D_MODEL), 0.02)
    ln1_b = nrm(ks[13], (DEPTH, D_MODEL), 0.02)
    w_ff1 = nrm(ks[14], (DEPTH, D_MODEL, D_FF), D_MODEL ** -0.5)
    w_ff2 = nrm(ks[15], (DEPTH, D_FF, D_MODEL), D_FF ** -0.5 * DEEPNORM_BETA)
    ln2_g = 1.0 + nrm(ks[16], (DEPTH, D_MODEL), 0.02)
    ln2_b = nrm(ks[17], (DEPTH, D_MODEL), 0.02)
    return {"x": x, "c": c, "w_cond": w_cond, "b_cond": b_cond, "w_in": w_in,
            "conv_w": conv_w, "conv_b": conv_b, "conv_ln_g": conv_ln_g, "conv_ln_b": conv_ln_b,
            "w_out": w_out, "ln1_g": ln1_g, "ln1_b": ln1_b, "w_ff1": w_ff1, "w_ff2": w_ff2,
            "ln2_g": ln2_g, "ln2_b": ln2_b}


def reference(x, c, w_cond, b_cond, w_in, conv_w, conv_b, conv_ln_g, conv_ln_b,
              w_out, ln1_g, ln1_b, w_ff1, w_ff2, ln2_g, ln2_b):
    bsz, seq, _ = x.shape
    split_at = [D_ATTN, 2 * D_ATTN, 3 * D_ATTN, 3 * D_ATTN + D_CONV]
    c_act = jax.nn.silu(c)
    for l in range(DEPTH):
        cond = jnp.einsum('bd,de->be', c_act, w_cond[l]) + b_cond[l]
        sh1, sc1, g1, sh2, sc2, g2 = jnp.split(cond, N_COND, axis=-1)

        u = modulate(layer_norm_plain(x), sh1, sc1)
        proj = jnp.einsum('bsd,de->bse', u, w_in[l])
        q, k, v, glu_val, glu_gate = jnp.split(proj, split_at, axis=-1)
        heads = lambda t: t.reshape(bsz, seq, N_ATTN_HEADS, HEAD_DIM)
        attn = stick_breaking_attention(heads(q), heads(k), heads(v)).reshape(bsz, seq, D_ATTN)
        conv = conformer_conv(glu_val, glu_gate, conv_w[l], conv_b[l], conv_ln_g[l], conv_ln_b[l])
        mix = jnp.einsum('bse,ed->bsd', jnp.concatenate([attn, conv], axis=-1), w_out[l])
        x = layer_norm(DEEPNORM_ALPHA * x + g1[:, None, :] * mix, ln1_g[l], ln1_b[l])

        u2 = modulate(layer_norm_plain(x), sh2, sc2)
        hid = jnp.square(jax.nn.relu(jnp.einsum('bsd,df->bsf', u2, w_ff1[l])))
        ff = jnp.einsum('bsf,fd->bsd', hid, w_ff2[l])
        x = layer_norm(DEEPNORM_ALPHA * x + g2[:, None, :] * ff, ln2_g[l], ln2_b[l])
    return x
```

```python
import functools
import math

import jax
import jax.numpy as jnp
from jax import lax
from jax.experimental import pallas as pl
from jax.experimental.pallas import tpu as pltpu

F32 = jnp.float32
BF16 = jnp.bfloat16

HEAD_DIM = 128
N_COND = 6
LN_EPS = 1e-5
LANES_V7X = 128
VMEM_BYTES_V7X = 64 * 1024 * 1024
LOG_UNDERFLOW_F32 = 104.0


def _ln_plain(x):
    mu = jnp.mean(x, axis=-1, keepdims=True)
    xc = x - mu
    var = jnp.mean(xc * xc, axis=-1, keepdims=True)
    return xc * lax.rsqrt(var + LN_EPS)


def _params(semantics, vmem_bytes):
    return pltpu.CompilerParams(
        dimension_semantics=semantics,
        vmem_limit_bytes=min(int(vmem_bytes), VMEM_BYTES_V7X),
    )


def _cond_kernel(ct_ref, w_ref, b_ref, o_ref):
    ct = ct_ref[...]
    act = ct * jax.nn.sigmoid(ct)
    w = w_ref[...]
    rows = [jnp.sum(w * act[:, b:b + 1], axis=0, keepdims=True) for b in range(ct.shape[1])]
    o_ref[...] = jnp.concatenate(rows, axis=0) + b_ref[...]


def _cond_call(c, w, b):
    bsz, d = c.shape
    n = w.shape[1]
    tn = 512
    vmem = 2 * d * tn * 4 * 2 + 4 * d * tn * 4
    return pl.pallas_call(
        _cond_kernel,
        grid=(n // tn,),
        in_specs=[
            pl.BlockSpec((d, bsz), lambda j: (0, 0)),
            pl.BlockSpec((d, tn), lambda j: (0, j)),
            pl.BlockSpec((1, tn), lambda j: (0, j)),
        ],
        out_specs=pl.BlockSpec((bsz, tn), lambda j: (0, j)),
        out_shape=jax.ShapeDtypeStruct((bsz, n), F32),
        compiler_params=_params(("arbitrary",), vmem),
        name="cond",
    )(c.T, w, b.reshape(1, n))


def _inproj_kernel(x_ref, sh_ref, sc_ref, w_ref, q_ref, k_ref, v_ref, g_ref, u_scr, val_scr, *, n_heads):
    j = pl.program_id(2)

    @pl.when(j == 0)
    def _():
        u = _ln_plain(x_ref[0]) * (1.0 + sc_ref[0]) + sh_ref[0]
        u_scr[...] = u.astype(BF16)

    res = jnp.dot(u_scr[...], w_ref[...], preferred_element_type=F32)

    def store_heads(ref):
        for h in range(n_heads):
            ref[0, h] = res[:, h * HEAD_DIM:(h + 1) * HEAD_DIM].astype(BF16)

    pl.when(j == 0)(lambda: store_heads(q_ref))
    pl.when(j == 1)(lambda: store_heads(k_ref))
    pl.when(j == 2)(lambda: store_heads(v_ref))

    @pl.when(j == 3)
    def _():
        val_scr[...] = res

    @pl.when(j == 4)
    def _():
        g_ref[0] = val_scr[...] * jax.nn.sigmoid(res)


def _inproj_call(x, sh, sc, w_in, *, tm):
    bsz, seq, d = x.shape
    d_attn = d // 2
    n_heads = d_attn // HEAD_DIM
    assert w_in.shape == (d, 5 * d_attn)
    qkv_shape = jax.ShapeDtypeStruct((bsz, n_heads, seq, HEAD_DIM), BF16)
    qkv_spec = pl.BlockSpec((1, n_heads, tm, HEAD_DIM), lambda b, i, j: (b, 0, i, 0))
    mod_spec = pl.BlockSpec((1, 1, d), lambda b, i, j: (b, 0, 0))
    vmem = (2 * tm * d * 4 + 2 * d * d_attn * 2 + 3 * 2 * tm * d_attn * 2 + 2 * tm * d_attn * 4
            + tm * d * 2 + tm * d_attn * 4 + 3 * tm * d * 4 + 2 * tm * d_attn * 4)
    return pl.pallas_call(
        functools.partial(_inproj_kernel, n_heads=n_heads),
        grid=(bsz, seq // tm, 5),
        in_specs=[
            pl.BlockSpec((1, tm, d), lambda b, i, j: (b, i, 0)),
            mod_spec, mod_spec,
            pl.BlockSpec((d, d_attn), lambda b, i, j: (0, j)),
        ],
        out_specs=[qkv_spec, qkv_spec, qkv_spec,
                   pl.BlockSpec((1, tm, d_attn), lambda b, i, j: (b, i, 0))],
        out_shape=[qkv_shape, qkv_shape, qkv_shape,
                   jax.ShapeDtypeStruct((bsz, seq, d_attn), F32)],
        scratch_shapes=[pltpu.VMEM((tm, d), BF16), pltpu.VMEM((tm, d_attn), F32)],
        compiler_params=_params(("parallel", "parallel", "arbitrary"), vmem),
        name="inproj",
    )(x, sh, sc, w_in)


def _attn_kernel(q_ref, k_ref, v_ref, o_ref, acc_ref, carry_ref, *, tq, hpg, scale):
    i = pl.program_id(2)
    row = lax.broadcasted_iota(jnp.int32, (tq, tq), 0)
    col = lax.broadcasted_iota(jnp.int32, (tq, tq), 1)
    causal = col < row
    ur = lax.broadcasted_iota(jnp.int32, (tq, tq + LANES_V7X), 0)
    uc = lax.broadcasted_iota(jnp.int32, (tq, tq + LANES_V7X), 1)
    sum_mat = jnp.where((ur > uc) | (uc >= tq), 1.0, 0.0).astype(BF16)

    def chunk(j, masked):
        start = pl.multiple_of(j * tq, tq)
        for h in range(hpg):
            q = q_ref[0, h]
            k = k_ref[0, h, pl.ds(start, tq), :]
            v = v_ref[0, h, pl.ds(start, tq), :]
            z = lax.dot_general(q, k, (((1,), (1,)), ((), ())), preferred_element_type=F32) * scale
            l = jnp.log1p(jnp.exp(-jnp.abs(z)))
            log_keep = -(jnp.maximum(z, 0.0) + l)
            log_beta = jnp.minimum(z, 0.0) - l
            if masked:
                log_keep = jnp.where(causal, log_keep, 0.0)
            hi = log_keep.astype(BF16)
            lo = (log_keep - hi.astype(F32)).astype(BF16)
            sums = (jnp.dot(hi, sum_mat, preferred_element_type=F32)
                    + jnp.dot(lo, sum_mat, preferred_element_type=F32))
            carry = carry_ref[h]
            log_a = log_beta + sums[:, :tq] + jnp.concatenate([carry] * (tq // LANES_V7X), axis=1)
            a = jnp.exp(log_a)
            if masked:
                a = jnp.where(causal, a, 0.0)
            acc_ref[h] += jnp.dot(a.astype(BF16), v, preferred_element_type=F32)
            carry_ref[h] = carry + sums[:, tq:]

    acc_ref[...] = jnp.zeros_like(acc_ref)
    carry_ref[...] = jnp.zeros_like(carry_ref)
    chunk(i, True)

    def keep_going(state):
        j, max_carry = state
        return jnp.logical_and(j >= 0, max_carry > -LOG_UNDERFLOW_F32)

    def sweep(state):
        j, _ = state
        chunk(j, False)
        return j - 1, jnp.max(jnp.max(carry_ref[...], axis=0))

    lax.while_loop(keep_going, sweep, (i - 1, jnp.float32(0.0)))

    for h in range(hpg):
        o_ref[0, :, h * HEAD_DIM:(h + 1) * HEAD_DIM] = acc_ref[h].astype(BF16)


def _attn_call(q, k, v, *, tq, hpg):
    bsz, n_heads, seq, hd = q.shape
    assert hd == HEAD_DIM and n_heads % hpg == 0 and seq % tq == 0
    kv_spec = pl.BlockSpec((1, hpg, seq, hd), lambda b, g, i: (b, g, 0, 0), pipeline_mode=pl.Buffered(1))
    vmem = (2 * hpg * seq * hd * 2 + 2 * hpg * tq * hd * 2 + 2 * tq * hpg * hd * 2
            + hpg * tq * hd * 4 + hpg * tq * LANES_V7X * 4 + 16 * tq * (tq + LANES_V7X) * 4)
    return pl.pallas_call(
        functools.partial(_attn_kernel, tq=tq, hpg=hpg, scale=1.0 / math.sqrt(hd)),
        grid=(bsz, n_heads // hpg, seq // tq),
        in_specs=[
            pl.BlockSpec((1, hpg, tq, hd), lambda b, g, i: (b, g, i, 0)),
            kv_spec, kv_spec,
        ],
        out_specs=pl.BlockSpec((1, tq, hpg * hd), lambda b, g, i: (b, i, g)),
        out_shape=jax.ShapeDtypeStruct((bsz, seq, n_heads * hd), BF16),
        scratch_shapes=[pltpu.VMEM((hpg, tq, hd), F32), pltpu.VMEM((hpg, tq, LANES_V7X), F32)],
        compiler_params=_params(("parallel", "parallel", "arbitrary"), vmem),
        name="attn",
    )(q, k, v)


def _conv_kernel(cur_ref, halo_ref, w_ref, b_ref, g_ref, beta_ref, o_ref, ext_scr, y_scr, *, ts, halo, width, rb):
    i = pl.program_id(1)
    n_ch = cur_ref.shape[2]
    prev = halo_ref[0]
    ext_scr[0:halo, :] = jnp.where(i == 0, jnp.zeros_like(prev), prev)
    ext_scr[halo:, :] = cur_ref[0]
    first = halo - (width - 1)

    def lane_block(c, carry):
        lanes = pl.ds(pl.multiple_of(c * LANES_V7X, LANES_V7X), LANES_V7X)
        for r in range(ts // rb):
            acc = jnp.broadcast_to(b_ref[:, lanes], (rb, LANES_V7X))
            for tap in range(width):
                acc = acc + w_ref[tap:tap + 1, lanes] * ext_scr[pl.ds(r * rb + first + tap, rb), lanes]
            y_scr[pl.ds(r * rb, rb), lanes] = acc
        return carry

    lax.fori_loop(0, n_ch // LANES_V7X, lane_block, 0)
    y = _ln_plain(y_scr[...]) * g_ref[...] + beta_ref[...]
    o_ref[0] = (y * jax.nn.sigmoid(y)).astype(o_ref.dtype)


def _conv_call(u, conv_w, conv_b, ln_g, ln_b, *, ts):
    bsz, seq, n_ch = u.shape
    width = conv_w.shape[0]
    halo = 32
    assert width - 1 <= halo and ts % halo == 0
    row = lambda a: a.reshape(1, n_ch)
    vec_spec = pl.BlockSpec((1, n_ch), lambda b, i: (0, 0))
    vmem = 2 * ts * n_ch * 4 + 2 * halo * n_ch * 4 + 2 * ts * n_ch * 2 + (2 * ts + halo) * n_ch * 4 + 4 * ts * n_ch * 4
    return pl.pallas_call(
        functools.partial(_conv_kernel, ts=ts, halo=halo, width=width, rb=128),
        grid=(bsz, seq // ts),
        in_specs=[
            pl.BlockSpec((1, ts, n_ch), lambda b, i: (b, i, 0)),
            pl.BlockSpec((1, halo, n_ch), lambda b, i: (b, jnp.maximum(i * (ts // halo) - 1, 0), 0)),
            pl.BlockSpec((width, n_ch), lambda b, i: (0, 0)),
            vec_spec, vec_spec, vec_spec,
        ],
        out_specs=pl.BlockSpec((1, ts, n_ch), lambda b, i: (b, i, 0)),
        out_shape=jax.ShapeDtypeStruct((bsz, seq, n_ch), BF16),
        scratch_shapes=[pltpu.VMEM((ts + halo, n_ch), F32), pltpu.VMEM((ts, n_ch), F32)],
        compiler_params=_params(("parallel", "parallel"), vmem),
        name="conv",
    )(u, u, conv_w.reshape(width, n_ch), row(conv_b), row(ln_g), row(ln_b))


def _outproj_kernel(attn_ref, conv_ref, x_ref, w_ref, g1_ref, sh_ref, sc_ref, lng_ref, lnb_ref,
                    x1_ref, u2_ref, *, alpha):
    d_attn = attn_ref.shape[2]
    mix = (jnp.dot(attn_ref[0], w_ref[:d_attn, :], preferred_element_type=F32)
           + jnp.dot(conv_ref[0], w_ref[d_attn:, :], preferred_element_type=F32))
    x1 = _ln_plain(alpha * x_ref[0] + g1_ref[0] * mix) * lng_ref[...] + lnb_ref[...]
    x1_ref[0] = x1
    u2_ref[0] = (_ln_plain(x1) * (1.0 + sc_ref[0]) + sh_ref[0]).astype(BF16)


def _outproj_call(attn, conv, x, w_out, g1, sh2, sc2, ln_g, ln_b, *, tm, alpha):
    bsz, seq, d = x.shape
    d_attn = attn.shape[2]
    half_spec = pl.BlockSpec((1, tm, d_attn), lambda b, i: (b, i, 0))
    full_spec = pl.BlockSpec((1, tm, d), lambda b, i: (b, i, 0))
    mod_spec = pl.BlockSpec((1, 1, d), lambda b, i: (b, 0, 0))
    vec_spec = pl.BlockSpec((1, d), lambda b, i: (0, 0))
    vmem = (2 * 2 * tm * d_attn * 2 + 2 * tm * d * 4 + d * d * 2 + 2 * tm * d * 4 + 2 * tm * d * 2
            + 6 * tm * d * 4)
    return pl.pallas_call(
        functools.partial(_outproj_kernel, alpha=alpha),
        grid=(bsz, seq // tm),
        in_specs=[
            half_spec, half_spec, full_spec,
            pl.BlockSpec((d, d), lambda b, i: (0, 0), pipeline_mode=pl.Buffered(1)),
            mod_spec, mod_spec, mod_spec, vec_spec, vec_spec,
        ],
        out_specs=[full_spec, full_spec],
        out_shape=[jax.ShapeDtypeStruct((bsz, seq, d), F32), jax.ShapeDtypeStruct((bsz, seq, d), BF16)],
        compiler_params=_params(("parallel", "parallel"), vmem),
        name="outproj",
    )(attn, conv, x, w_out, g1, sh2, sc2, ln_g.reshape(1, d), ln_b.reshape(1, d))


def _ffn_kernel(u_ref, w1_ref, w2_ref, x1_ref, g2_ref, lng_ref, lnb_ref, o_ref, acc_scr, *, alpha):
    f = pl.program_id(2)
    hid = jnp.dot(u_ref[0], w1_ref[...], preferred_element_type=F32)
    hid = jnp.square(jnp.maximum(hid, 0.0)).astype(BF16)
    part = jnp.dot(hid, w2_ref[...], preferred_element_type=F32)

    @pl.when(f == 0)
    def _():
        acc_scr[...] = part

    @pl.when(f > 0)
    def _():
        acc_scr[...] += part

    @pl.when(f == pl.num_programs(2) - 1)
    def _():
        y = alpha * x1_ref[0] + g2_ref[0] * acc_scr[...]
        o_ref[0] = _ln_plain(y) * lng_ref[...] + lnb_ref[...]


def _ffn_call(u2, w1, w2, x1, g2, ln_g, ln_b, *, tm, tf, alpha):
    bsz, seq, d = x1.shape
    d_ff = w1.shape[1]
    vec_spec = pl.BlockSpec((1, d), lambda b, i, f: (0, 0))
    vmem = (2 * tm * d * 2 + 2 * 2 * d * tf * 2 + 2 * tm * d * 4 + 2 * tm * d * 4 + tm * d * 4
            + tm * tf * 6 + 3 * tm * d * 4)
    return pl.pallas_call(
        functools.partial(_ffn_kernel, alpha=alpha),
        grid=(bsz, seq // tm, d_ff // tf),
        in_specs=[
            pl.BlockSpec((1, tm, d), lambda b, i, f: (b, i, 0)),
            pl.BlockSpec((d, tf), lambda b, i, f: (0, f)),
            pl.BlockSpec((tf, d), lambda b, i, f: (f, 0)),
            pl.BlockSpec((1, tm, d), lambda b, i, f: (b, i, 0)),
            pl.BlockSpec((1, 1, d), lambda b, i, f: (b, 0, 0)),
            vec_spec, vec_spec,
        ],
        out_specs=pl.BlockSpec((1, tm, d), lambda b, i, f: (b, i, 0)),
        out_shape=jax.ShapeDtypeStruct((bsz, seq, d), F32),
        scratch_shapes=[pltpu.VMEM((tm, d), F32)],
        compiler_params=_params(("parallel", "parallel", "arbitrary"), vmem),
        name="ffn",
    )(u2, w1, w2, x1, g2, ln_g.reshape(1, d), ln_b.reshape(1, d))


def kernel(x, c, w_cond, b_cond, w_in, conv_w, conv_b, conv_ln_g, conv_ln_b, w_out, ln1_g, ln1_b,
           w_ff1, w_ff2, ln2_g, ln2_b):
    bsz, seq, d = x.shape
    depth = w_cond.shape[0]
    alpha = (2.0 * depth) ** 0.25
    for l in range(depth):
        cond = _cond_call(c, w_cond[l], b_cond[l])
        sh1, sc1, g1, sh2, sc2, g2 = [m.reshape(bsz, 1, d) for m in jnp.split(cond, N_COND, axis=-1)]
        q, k, v, glu = _inproj_call(x, sh1, sc1, w_in[l].astype(BF16), tm=512)
        attn = _attn_call(q, k, v, tq=256, hpg=4)
        conv = _conv_call(glu, conv_w[l], conv_b[l], conv_ln_g[l], conv_ln_b[l], ts=512)
        x1, u2 = _outproj_call(attn, conv, x, w_out[l].astype(BF16), g1, sh2, sc2, ln1_g[l], ln1_b[l],
                               tm=512, alpha=alpha)
        x = _ffn_call(u2, w_ff1[l].astype(BF16), w_ff2[l].astype(BF16), x1, g2, ln2_g[l], ln2_b[l],
                      tm=512, tf=1024, alpha=alpha)
    return x
```

```python
import functools
import math

import jax
import jax.numpy as jnp
from jax import lax
from jax.experimental import pallas as pl
from jax.experimental.pallas import tpu as pltpu

F32 = jnp.float32
BF16 = jnp.bfloat16

HEAD_DIM = 128
N_COND = 6
LN_EPS = 1e-5
LANES_V7X = 128
SUBLANES_V7X = 8
VMEM_BYTES_V7X = 64 * 1024 * 1024
LOG_UNDERFLOW_F32 = 104.0


def _ln_plain(x):
    mu = jnp.mean(x, axis=-1, keepdims=True)
    xc = x - mu
    var = jnp.mean(xc * xc, axis=-1, keepdims=True)
    return xc * lax.rsqrt(var + LN_EPS)


def _params(semantics, vmem_bytes):
    return pltpu.CompilerParams(
        dimension_semantics=semantics,
        vmem_limit_bytes=min(int(vmem_bytes), VMEM_BYTES_V7X),
    )


def _cond_kernel(ct_ref, w_ref, b_ref, o_ref):
    ct = ct_ref[...]
    act = ct * jax.nn.sigmoid(ct)
    w = w_ref[...]
    rows = [jnp.sum(w * act[:, b:b + 1], axis=0, keepdims=True) for b in range(ct.shape[1])]
    o_ref[...] = jnp.concatenate(rows, axis=0) + b_ref[...]


def _cond_call(c, w, b):
    bsz, d = c.shape
    n = w.shape[1]
    tn = 512
    vmem = 2 * d * tn * 4 * 2 + 4 * d * tn * 4
    return pl.pallas_call(
        _cond_kernel,
        grid=(n // tn,),
        in_specs=[
            pl.BlockSpec((d, bsz), lambda j: (0, 0)),
            pl.BlockSpec((d, tn), lambda j: (0, j)),
            pl.BlockSpec((1, tn), lambda j: (0, j)),
        ],
        out_specs=pl.BlockSpec((bsz, tn), lambda j: (0, j)),
        out_shape=jax.ShapeDtypeStruct((bsz, n), F32),
        compiler_params=_params(("arbitrary",), vmem),
        name="cond",
    )(c.T, w, b.reshape(1, n))


def _inproj_kernel(x_ref, sh_ref, sc_ref, w_ref, q_ref, k_ref, v_ref, g_ref, u_scr, val_scr, *, n_heads):
    j = pl.program_id(2)

    @pl.when(j == 0)
    def _():
        u = _ln_plain(x_ref[0]) * (1.0 + sc_ref[0]) + sh_ref[0]
        u_scr[...] = u.astype(BF16)

    def project():
        return jnp.dot(u_scr[...], w_ref[...], preferred_element_type=F32)

    def store_heads(ref):
        res = project()
        for h in range(n_heads):
            ref[0, h] = res[:, h * HEAD_DIM:(h + 1) * HEAD_DIM].astype(BF16)

    pl.when(j == 0)(lambda: store_heads(q_ref))
    pl.when(j == 1)(lambda: store_heads(k_ref))
    pl.when(j == 2)(lambda: store_heads(v_ref))

    @pl.when(j == 3)
    def _():
        val_scr[...] = project()

    @pl.when(j == 4)
    def _():
        g_ref[0] = val_scr[...] * jax.nn.sigmoid(project())


def _inproj_call(x, sh, sc, w_in, *, tm):
    bsz, seq, d = x.shape
    d_attn = d // 2
    n_heads = d_attn // HEAD_DIM
    assert w_in.shape == (d, 5 * d_attn)
    qkv_shape = jax.ShapeDtypeStruct((bsz, n_heads, seq, HEAD_DIM), BF16)
    qkv_spec = pl.BlockSpec((1, n_heads, tm, HEAD_DIM), lambda b, i, j: (b, 0, i, 0))
    mod_spec = pl.BlockSpec((1, 1, d), lambda b, i, j: (b, 0, 0))
    vmem = (2 * tm * d * 4 + 2 * d * d_attn * 2 + 3 * 2 * tm * d_attn * 2 + 2 * tm * d_attn * 4
            + tm * d * 2 + tm * d_attn * 4 + 3 * tm * d * 4 + 2 * tm * d_attn * 4)
    return pl.pallas_call(
        functools.partial(_inproj_kernel, n_heads=n_heads),
        grid=(bsz, seq // tm, 5),
        in_specs=[
            pl.BlockSpec((1, tm, d), lambda b, i, j: (b, i, 0)),
            mod_spec, mod_spec,
            pl.BlockSpec((d, d_attn), lambda b, i, j: (0, j)),
        ],
        out_specs=[qkv_spec, qkv_spec, qkv_spec,
                   pl.BlockSpec((1, tm, d_attn), lambda b, i, j: (b, i, 0))],
        out_shape=[qkv_shape, qkv_shape, qkv_shape,
                   jax.ShapeDtypeStruct((bsz, seq, d_attn), F32)],
        scratch_shapes=[pltpu.VMEM((tm, d), BF16), pltpu.VMEM((tm, d_attn), F32)],
        compiler_params=_params(("parallel", "parallel", "arbitrary"), vmem),
        name="inproj",
    )(x, sh, sc, w_in)


def _attn_kernel(q_ref, k_ref, v_ref, o_ref, acc_ref, carry_ref, *, tq, hpg, scale):
    i = pl.program_id(2)
    row = lax.broadcasted_iota(jnp.int32, (tq, tq), 0)
    col = lax.broadcasted_iota(jnp.int32, (tq, tq), 1)
    causal = col < row
    later = (row > col).astype(BF16)

    def chunk(j, masked):
        start = pl.multiple_of(j * tq, tq)
        log_keeps, log_betas = [], []
        for h in range(hpg):
            q = q_ref[0, h]
            k = k_ref[0, h, pl.ds(start, tq), :]
            z = lax.dot_general(q, k, (((1,), (1,)), ((), ())), preferred_element_type=F32) * scale
            l = jnp.log(1.0 + jnp.exp(-jnp.abs(z)))
            log_keep = -(jnp.maximum(z, 0.0) + l)
            if masked:
                log_keep = jnp.where(causal, log_keep, 0.0)
            log_keeps.append(log_keep)
            log_betas.append(jnp.minimum(z, 0.0) - l)
        stacked = jnp.concatenate(log_keeps, axis=0)
        hi = stacked.astype(BF16)
        lo = (stacked - hi.astype(F32)).astype(BF16)
        sums = jnp.dot(jnp.concatenate([hi, lo], axis=0), later, preferred_element_type=F32)
        suffix = sums[:hpg * tq] + sums[hpg * tq:]
        for h in range(hpg):
            v = v_ref[0, h, pl.ds(start, tq), :]
            carry = carry_ref[h]
            a = jnp.exp(log_betas[h] + suffix[h * tq:(h + 1) * tq] + carry)
            if masked:
                a = jnp.where(causal, a, 0.0)
            acc_ref[h] += jnp.dot(a.astype(BF16), v, preferred_element_type=F32)
            carry_ref[h] = carry + jnp.sum(log_keeps[h], axis=1, keepdims=True)

    acc_ref[...] = jnp.zeros_like(acc_ref)
    carry_ref[...] = jnp.zeros_like(carry_ref)
    chunk(i, True)

    def keep_going(state):
        j, max_carry = state
        return jnp.logical_and(j >= 0, max_carry > -LOG_UNDERFLOW_F32)

    def sweep(state):
        j, _ = state
        chunk(j, False)
        return j - 1, jnp.max(carry_ref[...])

    lax.while_loop(keep_going, sweep, (i - 1, jnp.float32(0.0)))

    for h in range(hpg):
        o_ref[0, :, h * HEAD_DIM:(h + 1) * HEAD_DIM] = acc_ref[h].astype(BF16)


def _attn_call(q, k, v, *, tq, hpg):
    bsz, n_heads, seq, hd = q.shape
    assert hd == HEAD_DIM and n_heads % hpg == 0 and seq % tq == 0
    kv_spec = pl.BlockSpec((1, hpg, seq, hd), lambda b, g, i: (b, g, 0, 0), pipeline_mode=pl.Buffered(1))
    vmem = (2 * hpg * seq * hd * 2 + 2 * hpg * tq * hd * 2 + 2 * tq * hpg * hd * 2
            + hpg * tq * hd * 4 + hpg * tq * LANES_V7X * 4 + 12 * hpg * tq * tq * 4)
    return pl.pallas_call(
        functools.partial(_attn_kernel, tq=tq, hpg=hpg, scale=1.0 / math.sqrt(hd)),
        grid=(bsz, n_heads // hpg, seq // tq),
        in_specs=[
            pl.BlockSpec((1, hpg, tq, hd), lambda b, g, i: (b, g, i, 0)),
            kv_spec, kv_spec,
        ],
        out_specs=pl.BlockSpec((1, tq, hpg * hd), lambda b, g, i: (b, i, g)),
        out_shape=jax.ShapeDtypeStruct((bsz, seq, n_heads * hd), BF16),
        scratch_shapes=[pltpu.VMEM((hpg, tq, hd), F32), pltpu.VMEM((hpg, tq, 1), F32)],
        compiler_params=_params(("parallel", "parallel", "arbitrary"), vmem),
        name="attn",
    )(q, k, v)


def _conv_kernel(cur_ref, halo_ref, w_ref, b_ref, g_ref, beta_ref, o_ref, ext_scr, shift_scr, y_scr,
                 *, ts, halo, width, rb):
    i = pl.program_id(1)
    n_ch = cur_ref.shape[2]
    prev = halo_ref[0]
    ext_scr[0:halo, :] = jnp.where(i == 0, jnp.zeros_like(prev), prev)
    ext_scr[halo:, :] = cur_ref[0]
    first = halo - (width - 1)
    n_shift = shift_scr.shape[1]

    def lane_block(c, carry):
        lanes = pl.ds(pl.multiple_of(c * LANES_V7X, LANES_V7X), LANES_V7X)
        for m in range(1, SUBLANES_V7X):
            shift_scr[m - 1] = ext_scr[pl.ds(m, n_shift), lanes]
        for r in range(ts // rb):
            acc = jnp.broadcast_to(b_ref[:, lanes], (rb, LANES_V7X))
            for tap in range(width):
                m = (first + tap) % SUBLANES_V7X
                rows = pl.ds(r * rb + first + tap - m, rb)
                src = ext_scr[rows, lanes] if m == 0 else shift_scr[m - 1, rows, :]
                acc = acc + w_ref[tap:tap + 1, lanes] * src
            y_scr[pl.ds(r * rb, rb), lanes] = acc
        return carry

    lax.fori_loop(0, n_ch // LANES_V7X, lane_block, 0)
    y = _ln_plain(y_scr[...]) * g_ref[...] + beta_ref[...]
    o_ref[0] = (y * jax.nn.sigmoid(y)).astype(o_ref.dtype)


def _conv_call(u, conv_w, conv_b, ln_g, ln_b, *, ts):
    bsz, seq, n_ch = u.shape
    width = conv_w.shape[0]
    halo = 32
    assert width - 1 <= halo and ts % halo == 0
    row = lambda a: a.reshape(1, n_ch)
    vec_spec = pl.BlockSpec((1, n_ch), lambda b, i: (0, 0))
    n_shift = ts + halo - SUBLANES_V7X
    vmem = (2 * ts * n_ch * 4 + 2 * halo * n_ch * 4 + 2 * ts * n_ch * 2 + (2 * ts + halo) * n_ch * 4
            + (SUBLANES_V7X - 1) * n_shift * LANES_V7X * 4 + 4 * ts * n_ch * 4)
    return pl.pallas_call(
        functools.partial(_conv_kernel, ts=ts, halo=halo, width=width, rb=128),
        grid=(bsz, seq // ts),
        in_specs=[
            pl.BlockSpec((1, ts, n_ch), lambda b, i: (b, i, 0)),
            pl.BlockSpec((1, halo, n_ch), lambda b, i: (b, jnp.maximum(i * (ts // halo) - 1, 0), 0)),
            pl.BlockSpec((width, n_ch), lambda b, i: (0, 0)),
            vec_spec, vec_spec, vec_spec,
        ],
        out_specs=pl.BlockSpec((1, ts, n_ch), lambda b, i: (b, i, 0)),
        out_shape=jax.ShapeDtypeStruct((bsz, seq, n_ch), BF16),
        scratch_shapes=[pltpu.VMEM((ts + halo, n_ch), F32),
                        pltpu.VMEM((SUBLANES_V7X - 1, n_shift, LANES_V7X), F32),
                        pltpu.VMEM((ts, n_ch), F32)],
        compiler_params=_params(("parallel", "parallel"), vmem),
        name="conv",
    )(u, u, conv_w.reshape(width, n_ch), row(conv_b), row(ln_g), row(ln_b))


def _outproj_kernel(attn_ref, conv_ref, x_ref, w_ref, g1_ref, sh_ref, sc_ref, lng_ref, lnb_ref,
                    x1_ref, u2_ref, *, alpha, sub):
    d_attn = attn_ref.shape[2]
    for s in range(attn_ref.shape[1] // sub):
        rows = pl.ds(s * sub, sub)
        mix = (jnp.dot(attn_ref[0, rows, :], w_ref[:d_attn, :], preferred_element_type=F32)
               + jnp.dot(conv_ref[0, rows, :], w_ref[d_attn:, :], preferred_element_type=F32))
        x1 = _ln_plain(alpha * x_ref[0, rows, :] + g1_ref[0] * mix) * lng_ref[...] + lnb_ref[...]
        x1_ref[0, rows, :] = x1
        u2_ref[0, rows, :] = (_ln_plain(x1) * (1.0 + sc_ref[0]) + sh_ref[0]).astype(BF16)


def _outproj_call(attn, conv, x, w_out, g1, sh2, sc2, ln_g, ln_b, *, tm, alpha):
    bsz, seq, d = x.shape
    d_attn = attn.shape[2]
    half_spec = pl.BlockSpec((1, tm, d_attn), lambda b, i: (b, i, 0))
    full_spec = pl.BlockSpec((1, tm, d), lambda b, i: (b, i, 0))
    mod_spec = pl.BlockSpec((1, 1, d), lambda b, i: (b, 0, 0))
    vec_spec = pl.BlockSpec((1, d), lambda b, i: (0, 0))
    vmem = (2 * 2 * tm * d_attn * 2 + 2 * tm * d * 4 + d * d * 2 + 2 * tm * d * 4 + 2 * tm * d * 2
            + 6 * tm * d * 4)
    return pl.pallas_call(
        functools.partial(_outproj_kernel, alpha=alpha, sub=128),
        grid=(bsz, seq // tm),
        in_specs=[
            half_spec, half_spec, full_spec,
            pl.BlockSpec((d, d), lambda b, i: (0, 0), pipeline_mode=pl.Buffered(1)),
            mod_spec, mod_spec, mod_spec, vec_spec, vec_spec,
        ],
        out_specs=[full_spec, full_spec],
        out_shape=[jax.ShapeDtypeStruct((bsz, seq, d), F32), jax.ShapeDtypeStruct((bsz, seq, d), BF16)],
        compiler_params=_params(("parallel", "parallel"), vmem),
        name="outproj",
    )(attn, conv, x, w_out, g1, sh2, sc2, ln_g.reshape(1, d), ln_b.reshape(1, d))


def _ffn_kernel(u_ref, w1_ref, w2_ref, x1_ref, g2_ref, lng_ref, lnb_ref, o_ref, acc_scr, *, alpha):
    f = pl.program_id(2)

    @pl.when(f == 0)
    def _():
        acc_scr[...] = jnp.zeros_like(acc_scr)

    hid = jnp.dot(u_ref[0], w1_ref[...], preferred_element_type=F32)
    hid = jnp.square(jnp.maximum(hid, 0.0)).astype(BF16)
    acc_scr[...] += jnp.dot(hid, w2_ref[...], preferred_element_type=F32)

    @pl.when(f == pl.num_programs(2) - 1)
    def _():
        y = alpha * x1_ref[0] + g2_ref[0] * acc_scr[...]
        o_ref[0] = _ln_plain(y) * lng_ref[...] + lnb_ref[...]


def _ffn_call(u2, w1, w2, x1, g2, ln_g, ln_b, *, tm, tf, alpha):
    bsz, seq, d = x1.shape
    d_ff = w1.shape[1]
    vec_spec = pl.BlockSpec((1, d), lambda b, i, f: (0, 0))
    vmem = (2 * tm * d * 2 + 2 * 2 * d * tf * 2 + 2 * tm * d * 4 + 2 * tm * d * 4 + tm * d * 4
            + tm * tf * 6 + 3 * tm * d * 4)
    return pl.pallas_call(
        functools.partial(_ffn_kernel, alpha=alpha),
        grid=(bsz, seq // tm, d_ff // tf),
        in_specs=[
            pl.BlockSpec((1, tm, d), lambda b, i, f: (b, i, 0)),
            pl.BlockSpec((d, tf), lambda b, i, f: (0, f)),
            pl.BlockSpec((tf, d), lambda b, i, f: (f, 0)),
            pl.BlockSpec((1, tm, d), lambda b, i, f: (b, i, 0)),
            pl.BlockSpec((1, 1, d), lambda b, i, f: (b, 0, 0)),
            vec_spec, vec_spec,
        ],
        out_specs=pl.BlockSpec((1, tm, d), lambda b, i, f: (b, i, 0)),
        out_shape=jax.ShapeDtypeStruct((bsz, seq, d), F32),
        scratch_shapes=[pltpu.VMEM((tm, d), F32)],
        compiler_params=_params(("parallel", "parallel", "arbitrary"), vmem),
        name="ffn",
    )(u2, w1, w2, x1, g2, ln_g.reshape(1, d), ln_b.reshape(1, d))


def kernel(x, c, w_cond, b_cond, w_in, conv_w, conv_b, conv_ln_g, conv_ln_b, w_out, ln1_g, ln1_b,
           w_ff1, w_ff2, ln2_g, ln2_b):
    bsz, seq, d = x.shape
    depth = w_cond.shape[0]
    alpha = (2.0 * depth) ** 0.25
    for l in range(depth):
        cond = _cond_call(c, w_cond[l], b_cond[l])
        sh1, sc1, g1, sh2, sc2, g2 = [m.reshape(bsz, 1, d) for m in jnp.split(cond, N_COND, axis=-1)]
        q, k, v, glu = _inproj_call(x, sh1, sc1, w_in[l].astype(BF16), tm=512)
        attn = _attn_call(q, k, v, tq=256, hpg=4)
        conv = _conv_call(glu, conv_w[l], conv_b[l], conv_ln_g[l], conv_ln_b[l], ts=512)
        x1, u2 = _outproj_call(attn, conv, x, w_out[l].astype(BF16), g1, sh2, sc2, ln1_g[l], ln1_b[l],
                               tm=512, alpha=alpha)
        x = _ffn_call(u2, w_ff1[l].astype(BF16), w_ff2[l].astype(BF16), x1, g2, ln2_g[l], ln2_b[l],
                      tm=512, tf=1024, alpha=alpha)
    return x
```

```python
import functools
import math

import jax
import jax.numpy as jnp
from jax import lax
from jax.experimental import pallas as pl
from jax.experimental.pallas import tpu as pltpu

F32 = jnp.float32
BF16 = jnp.bfloat16

HEAD_DIM = 128
N_COND = 6
LN_EPS = 1e-5
LANES_V7X = 128
SUBLANES_V7X = 8
VMEM_BYTES_V7X = 64 * 1024 * 1024
LOG_UNDERFLOW_F32 = 104.0


def _ln_plain(x):
    mu = jnp.mean(x, axis=-1, keepdims=True)
    xc = x - mu
    var = jnp.mean(xc * xc, axis=-1, keepdims=True)
    return xc * lax.rsqrt(var + LN_EPS)


def _params(semantics, vmem_bytes):
    return pltpu.CompilerParams(
        dimension_semantics=semantics,
        vmem_limit_bytes=min(int(vmem_bytes), VMEM_BYTES_V7X),
    )


def _cond_kernel(ct_ref, w_ref, b_ref, o_ref):
    ct = ct_ref[...]
    act = ct * jax.nn.sigmoid(ct)
    w = w_ref[...]
    rows = [jnp.sum(w * act[:, b:b + 1], axis=0, keepdims=True) for b in range(ct.shape[1])]
    o_ref[...] = jnp.concatenate(rows, axis=0) + b_ref[...]


def _cond_call(c, w, b):
    bsz, d = c.shape
    n = w.shape[1]
    tn = 512
    vmem = 2 * d * tn * 4 * 2 + 4 * d * tn * 4
    return pl.pallas_call(
        _cond_kernel,
        grid=(n // tn,),
        in_specs=[
            pl.BlockSpec((d, bsz), lambda j: (0, 0)),
            pl.BlockSpec((d, tn), lambda j: (0, j)),
            pl.BlockSpec((1, tn), lambda j: (0, j)),
        ],
        out_specs=pl.BlockSpec((bsz, tn), lambda j: (0, j)),
        out_shape=jax.ShapeDtypeStruct((bsz, n), F32),
        compiler_params=_params(("arbitrary",), vmem),
        name="cond",
    )(c.T, w, b.reshape(1, n))


def _inproj_kernel(x_ref, sh_ref, sc_ref, w_ref, q_ref, k_ref, v_ref, g_ref, u_scr, val_scr, *, n_heads):
    j = pl.program_id(2)

    @pl.when(j == 0)
    def _():
        u = _ln_plain(x_ref[0]) * (1.0 + sc_ref[0]) + sh_ref[0]
        u_scr[...] = u.astype(BF16)

    def project():
        return jnp.dot(u_scr[...], w_ref[...], preferred_element_type=F32)

    def store_heads(ref):
        res = project()
        for h in range(n_heads):
            ref[0, h] = res[:, h * HEAD_DIM:(h + 1) * HEAD_DIM].astype(BF16)

    pl.when(j == 0)(lambda: store_heads(q_ref))
    pl.when(j == 1)(lambda: store_heads(k_ref))
    pl.when(j == 2)(lambda: store_heads(v_ref))

    @pl.when(j == 3)
    def _():
        val_scr[...] = project()

    @pl.when(j == 4)
    def _():
        g_ref[0] = val_scr[...] * jax.nn.sigmoid(project())


def _inproj_call(x, sh, sc, w_in, *, tm):
    bsz, seq, d = x.shape
    d_attn = d // 2
    n_heads = d_attn // HEAD_DIM
    assert w_in.shape == (d, 5 * d_attn)
    qkv_shape = jax.ShapeDtypeStruct((bsz, n_heads, seq, HEAD_DIM), BF16)
    qkv_spec = pl.BlockSpec((1, n_heads, tm, HEAD_DIM), lambda b, i, j: (b, 0, i, 0))
    mod_spec = pl.BlockSpec((1, 1, d), lambda b, i, j: (b, 0, 0))
    vmem = (2 * tm * d * 4 + 2 * d * d_attn * 2 + 3 * 2 * tm * d_attn * 2 + 2 * tm * d_attn * 4
            + tm * d * 2 + tm * d_attn * 4 + 3 * tm * d * 4 + 2 * tm * d_attn * 4)
    return pl.pallas_call(
        functools.partial(_inproj_kernel, n_heads=n_heads),
        grid=(bsz, seq // tm, 5),
        in_specs=[
            pl.BlockSpec((1, tm, d), lambda b, i, j: (b, i, 0)),
            mod_spec, mod_spec,
            pl.BlockSpec((d, d_attn), lambda b, i, j: (0, j)),
        ],
        out_specs=[qkv_spec, qkv_spec, qkv_spec,
                   pl.BlockSpec((1, tm, d_attn), lambda b, i, j: (b, i, 0))],
        out_shape=[qkv_shape, qkv_shape, qkv_shape,
                   jax.ShapeDtypeStruct((bsz, seq, d_attn), F32)],
        scratch_shapes=[pltpu.VMEM((tm, d), BF16), pltpu.VMEM((tm, d_attn), F32)],
        compiler_params=_params(("parallel", "parallel", "arbitrary"), vmem),
        name="inproj",
    )(x, sh, sc, w_in)


def _attn_kernel(q_ref, k_ref, v_ref, o_ref, acc_ref, carry_ref, *, tq, hpg, scale):
    i = pl.program_id(2)
    row = lax.broadcasted_iota(jnp.int32, (tq, tq), 0)
    col = lax.broadcasted_iota(jnp.int32, (tq, tq), 1)
    causal = col < row
    from_here = (row >= col).astype(BF16)
    sign_bit = jnp.int32(-2 ** 31)

    def chunk(j, masked):
        start = pl.multiple_of(j * tq, tq)
        zs, log_keeps = [], []
        for h in range(hpg):
            q = q_ref[0, h]
            k = k_ref[0, h, pl.ds(start, tq), :]
            z = lax.dot_general(q, k, (((1,), (1,)), ((), ())), preferred_element_type=F32) * scale
            neg_abs = lax.bitcast_convert_type(lax.bitcast_convert_type(z, jnp.int32) | sign_bit, F32)
            log_keep = -(jnp.maximum(z, 0.0) + jnp.log(1.0 + jnp.exp(neg_abs)))
            if masked:
                log_keep = jnp.where(causal, log_keep, 0.0)
            zs.append(z)
            log_keeps.append(log_keep)
        halves = []
        for log_keep in log_keeps:
            hi = log_keep.astype(BF16)
            halves += [hi, (log_keep - hi.astype(F32)).astype(BF16)]
        sums = jnp.dot(jnp.concatenate(halves, axis=0), from_here, preferred_element_type=F32)
        for h in range(hpg):
            v = v_ref[0, h, pl.ds(start, tq), :]
            carry = carry_ref[h]
            base = 2 * h * tq
            inclusive = sums[base:base + tq] + sums[base + tq:base + 2 * tq]
            a = jnp.exp(zs[h] + inclusive + carry)
            if masked:
                a = jnp.where(causal, a, 0.0)
            acc_ref[h] += jnp.dot(a.astype(BF16), v, preferred_element_type=F32)
            carry_ref[h] = carry + jnp.sum(log_keeps[h], axis=1, keepdims=True)

    acc_ref[...] = jnp.zeros_like(acc_ref)
    carry_ref[...] = jnp.zeros_like(carry_ref)

    @pl.when(i == 0)
    def _():
        chunk(i, True)

    @pl.when(i > 0)
    def _():
        chunk(i, True)
        chunk(i - 1, False)

    def keep_going(state):
        j, max_carry = state
        return jnp.logical_and(j >= 0, max_carry > -LOG_UNDERFLOW_F32)

    def sweep(state):
        j, _ = state
        chunk(j, False)
        return j - 1, jnp.max(carry_ref[...])

    lax.while_loop(keep_going, sweep, (i - 2, jnp.max(carry_ref[...])))

    for h in range(hpg):
        o_ref[0, :, h * HEAD_DIM:(h + 1) * HEAD_DIM] = acc_ref[h].astype(BF16)


def _attn_call(q, k, v, *, tq, hpg):
    bsz, n_heads, seq, hd = q.shape
    assert hd == HEAD_DIM and n_heads % hpg == 0 and seq % tq == 0
    kv_spec = pl.BlockSpec((1, hpg, seq, hd), lambda b, g, i: (b, g, 0, 0), pipeline_mode=pl.Buffered(1))
    vmem = (2 * hpg * seq * hd * 2 + 2 * hpg * tq * hd * 2 + 2 * tq * hpg * hd * 2
            + hpg * tq * hd * 4 + hpg * tq * LANES_V7X * 4 + 12 * hpg * tq * tq * 4)
    return pl.pallas_call(
        functools.partial(_attn_kernel, tq=tq, hpg=hpg, scale=1.0 / math.sqrt(hd)),
        grid=(bsz, n_heads // hpg, seq // tq),
        in_specs=[
            pl.BlockSpec((1, hpg, tq, hd), lambda b, g, i: (b, g, i, 0)),
            kv_spec, kv_spec,
        ],
        out_specs=pl.BlockSpec((1, tq, hpg * hd), lambda b, g, i: (b, i, g)),
        out_shape=jax.ShapeDtypeStruct((bsz, seq, n_heads * hd), BF16),
        scratch_shapes=[pltpu.VMEM((hpg, tq, hd), F32), pltpu.VMEM((hpg, tq, 1), F32)],
        compiler_params=_params(("parallel", "parallel", "arbitrary"), vmem),
        name="attn",
    )(q, k, v)


def _conv_kernel(cur_ref, halo_ref, w_ref, b_ref, g_ref, beta_ref, o_ref, ext_scr, shift_scr, y_scr,
                 *, ts, halo, width, rb):
    i = pl.program_id(1)
    n_ch = cur_ref.shape[2]
    prev = halo_ref[0]
    ext_scr[0:halo, :] = jnp.where(i == 0, jnp.zeros_like(prev), prev)
    ext_scr[halo:, :] = cur_ref[0]
    first = halo - (width - 1)
    n_shift = shift_scr.shape[1]

    def lane_block(c, carry):
        lanes = pl.ds(pl.multiple_of(c * LANES_V7X, LANES_V7X), LANES_V7X)
        for m in range(1, SUBLANES_V7X):
            shift_scr[m - 1] = ext_scr[pl.ds(m, n_shift), lanes]
        for r in range(ts // rb):
            acc = jnp.broadcast_to(b_ref[:, lanes], (rb, LANES_V7X))
            for tap in range(width):
                m = (first + tap) % SUBLANES_V7X
                rows = pl.ds(r * rb + first + tap - m, rb)
                src = ext_scr[rows, lanes] if m == 0 else shift_scr[m - 1, rows, :]
                acc = acc + w_ref[tap:tap + 1, lanes] * src
            y_scr[pl.ds(r * rb, rb), lanes] = acc
        return carry

    lax.fori_loop(0, n_ch // LANES_V7X, lane_block, 0)
    y = _ln_plain(y_scr[...]) * g_ref[...] + beta_ref[...]
    o_ref[0] = (y * jax.nn.sigmoid(y)).astype(o_ref.dtype)


def _conv_call(u, conv_w, conv_b, ln_g, ln_b, *, ts):
    bsz, seq, n_ch = u.shape
    width = conv_w.shape[0]
    halo = 32
    assert width - 1 <= halo and ts % halo == 0
    row = lambda a: a.reshape(1, n_ch)
    vec_spec = pl.BlockSpec((1, n_ch), lambda b, i: (0, 0))
    n_shift = ts + halo - SUBLANES_V7X
    vmem = (2 * ts * n_ch * 4 + 2 * halo * n_ch * 4 + 2 * ts * n_ch * 2 + (2 * ts + halo) * n_ch * 4
            + (SUBLANES_V7X - 1) * n_shift * LANES_V7X * 4 + 4 * ts * n_ch * 4)
    return pl.pallas_call(
        functools.partial(_conv_kernel, ts=ts, halo=halo, width=width, rb=128),
        grid=(bsz, seq // ts),
        in_specs=[
            pl.BlockSpec((1, ts, n_ch), lambda b, i: (b, i, 0)),
            pl.BlockSpec((1, halo, n_ch), lambda b, i: (b, jnp.maximum(i * (ts // halo) - 1, 0), 0)),
            pl.BlockSpec((width, n_ch), lambda b, i: (0, 0)),
            vec_spec, vec_spec, vec_spec,
        ],
        out_specs=pl.BlockSpec((1, ts, n_ch), lambda b, i: (b, i, 0)),
        out_shape=jax.ShapeDtypeStruct((bsz, seq, n_ch), BF16),
        scratch_shapes=[pltpu.VMEM((ts + halo, n_ch), F32),
                        pltpu.VMEM((SUBLANES_V7X - 1, n_shift, LANES_V7X), F32),
                        pltpu.VMEM((ts, n_ch), F32)],
        compiler_params=_params(("parallel", "parallel"), vmem),
        name="conv",
    )(u, u, conv_w.reshape(width, n_ch), row(conv_b), row(ln_g), row(ln_b))


def _outproj_kernel(attn_ref, conv_ref, x_ref, w_ref, g1_ref, sh_ref, sc_ref, lng_ref, lnb_ref,
                    x1_ref, u2_ref, *, alpha, sub):
    d_attn = attn_ref.shape[2]
    for s in range(attn_ref.shape[1] // sub):
        rows = pl.ds(s * sub, sub)
        mix = (jnp.dot(attn_ref[0, rows, :], w_ref[:d_attn, :], preferred_element_type=F32)
               + jnp.dot(conv_ref[0, rows, :], w_ref[d_attn:, :], preferred_element_type=F32))
        x1 = _ln_plain(alpha * x_ref[0, rows, :] + g1_ref[0] * mix) * lng_ref[...] + lnb_ref[...]
        x1_ref[0, rows, :] = x1
        u2_ref[0, rows, :] = (_ln_plain(x1) * (1.0 + sc_ref[0]) + sh_ref[0]).astype(BF16)


def _outproj_call(attn, conv, x, w_out, g1, sh2, sc2, ln_g, ln_b, *, tm, alpha):
    bsz, seq, d = x.shape
    d_attn = attn.shape[2]
    half_spec = pl.BlockSpec((1, tm, d_attn), lambda b, i: (b, i, 0))
    full_spec = pl.BlockSpec((1, tm, d), lambda b, i: (b, i, 0))
    mod_spec = pl.BlockSpec((1, 1, d), lambda b, i: (b, 0, 0))
    vec_spec = pl.BlockSpec((1, d), lambda b, i: (0, 0))
    vmem = (2 * 2 * tm * d_attn * 2 + 2 * tm * d * 4 + d * d * 2 + 2 * tm * d * 4 + 2 * tm * d * 2
            + 6 * tm * d * 4)
    return pl.pallas_call(
        functools.partial(_outproj_kernel, alpha=alpha, sub=128),
        grid=(bsz, seq // tm),
        in_specs=[
            half_spec, half_spec, full_spec,
            pl.BlockSpec((d, d), lambda b, i: (0, 0), pipeline_mode=pl.Buffered(1)),
            mod_spec, mod_spec, mod_spec, vec_spec, vec_spec,
        ],
        out_specs=[full_spec, full_spec],
        out_shape=[jax.ShapeDtypeStruct((bsz, seq, d), F32), jax.ShapeDtypeStruct((bsz, seq, d), BF16)],
        compiler_params=_params(("parallel", "parallel"), vmem),
        name="outproj",
    )(attn, conv, x, w_out, g1, sh2, sc2, ln_g.reshape(1, d), ln_b.reshape(1, d))


def _ffn_kernel(u_ref, w1_ref, w2_ref, x1_ref, g2_ref, lng_ref, lnb_ref, o_ref, *, alpha):
    f = pl.program_id(2)
    cw = x1_ref.shape[2]

    @pl.when(f == 0)
    def _():
        o_ref[...] = jnp.zeros_like(o_ref)

    hid = jnp.dot(u_ref[0], w1_ref[...], preferred_element_type=F32)
    hid = jnp.square(jnp.maximum(hid, 0.0)).astype(BF16)
    o_ref[0] += g2_ref[0] * jnp.dot(hid, w2_ref[...], preferred_element_type=F32)
    cols = pl.ds(pl.multiple_of(f * cw, cw), cw)
    o_ref[0, :, cols] += alpha * x1_ref[0]

    @pl.when(f == pl.num_programs(2) - 1)
    def _():
        o_ref[0] = _ln_plain(o_ref[0]) * lng_ref[...] + lnb_ref[...]


def _ffn_call(u2, w1, w2, x1, g2, ln_g, ln_b, *, tm, tf, alpha):
    bsz, seq, d = x1.shape
    d_ff = w1.shape[1]
    n_f = d_ff // tf
    cw = d // n_f
    assert cw % LANES_V7X == 0
    vec_spec = pl.BlockSpec((1, d), lambda b, i, f: (0, 0))
    vmem = (2 * tm * d * 2 + 2 * 2 * d * tf * 2 + 2 * tm * cw * 4 + 2 * tm * d * 4
            + tm * tf * 6 + 2 * tm * d * 4)
    return pl.pallas_call(
        functools.partial(_ffn_kernel, alpha=alpha),
        grid=(bsz, seq // tm, n_f),
        in_specs=[
            pl.BlockSpec((1, tm, d), lambda b, i, f: (b, i, 0)),
            pl.BlockSpec((d, tf), lambda b, i, f: (0, f)),
            pl.BlockSpec((tf, d), lambda b, i, f: (f, 0)),
            pl.BlockSpec((1, tm, cw), lambda b, i, f: (b, i, f)),
            pl.BlockSpec((1, 1, d), lambda b, i, f: (b, 0, 0)),
            vec_spec, vec_spec,
        ],
        out_specs=pl.BlockSpec((1, tm, d), lambda b, i, f: (b, i, 0)),
        out_shape=jax.ShapeDtypeStruct((bsz, seq, d), F32),
        compiler_params=_params(("parallel", "parallel", "arbitrary"), vmem),
        name="ffn",
    )(u2, w1, w2, x1, g2, ln_g.reshape(1, d), ln_b.reshape(1, d))


def kernel(x, c, w_cond, b_cond, w_in, conv_w, conv_b, conv_ln_g, conv_ln_b, w_out, ln1_g, ln1_b,
           w_ff1, w_ff2, ln2_g, ln2_b):
    bsz, seq, d = x.shape
    depth = w_cond.shape[0]
    alpha = (2.0 * depth) ** 0.25
    for l in range(depth):
        cond = _cond_call(c, w_cond[l], b_cond[l])
        sh1, sc1, g1, sh2, sc2, g2 = [m.reshape(bsz, 1, d) for m in jnp.split(cond, N_COND, axis=-1)]
        q, k, v, glu = _inproj_call(x, sh1, sc1, w_in[l].astype(BF16), tm=1024)
        attn = _attn_call(q, k, v, tq=256, hpg=4)
        conv = _conv_call(glu, conv_w[l], conv_b[l], conv_ln_g[l], conv_ln_b[l], ts=512)
        x1, u2 = _outproj_call(attn, conv, x, w_out[l].astype(BF16), g1, sh2, sc2, ln1_g[l], ln1_b[l],
                               tm=512, alpha=alpha)
        x = _ffn_call(u2, w_ff1[l].astype(BF16), w_ff2[l].astype(BF16), x1, g2, ln2_g[l], ln2_b[l],
                      tm=1024, tf=1024, alpha=alpha)
    return x
```
